```python
import math
import jax, jax.numpy as jnp
from jax import lax
import numpy as np

D_MODEL = 1024
BATCH = 16
SEQ = 4096
DEPTH = 2

GRID_W = 64
ATTN_WIDTH = D_MODEL // 2
POOL_WIDTH = D_MODEL - ATTN_WIDTH
HEAD_DIM = 64
N_HEADS = ATTN_WIDTH // HEAD_DIM
WIN_H_MAX = 8
WIN_W = 16
POOL_WINDOWS = (2, 4, 8, 16)
N_POOL_GROUPS = len(POOL_WINDOWS)
POOL_GROUP = POOL_WIDTH // N_POOL_GROUPS
PROJ_WIDTH = 3 * ATTN_WIDTH + POOL_WIDTH
D_FF = 4 * D_MODEL
N_MOD = 6
DN_ALPHA = (2.0 * DEPTH) ** 0.25
DN_BETA = (8.0 * DEPTH) ** -0.25
LN_EPS = 1e-5

kernel_name = "hybrid_natten_pool_deepnorm_encoder"


def layer_norm(x, g, b):
    xf = x.astype(jnp.float32)
    mu = jnp.mean(xf, axis=-1, keepdims=True)
    var = jnp.mean(jnp.square(xf - mu), axis=-1, keepdims=True)
    y = (xf - mu) * lax.rsqrt(var + LN_EPS)
    return (y * g.astype(jnp.float32) + b.astype(jnp.float32)).astype(x.dtype)


def neighborhood_attention(q, k, v, rpb):
    b, s, h, dh = q.shape
    rows = s // GRID_W
    kh = min(WIN_H_MAX, rows)
    qg = q.reshape(b, rows, GRID_W, h, dh).transpose(1, 0, 3, 2, 4)
    kg = k.reshape(b, rows, GRID_W, h, dh).transpose(0, 3, 1, 2, 4)
    vg = v.reshape(b, rows, GRID_W, h, dh).transpose(0, 3, 1, 2, 4)
    col = np.arange(GRID_W)
    col_start = np.clip(col - WIN_W // 2, 0, GRID_W - WIN_W)
    col_idx = col_start[:, None] + np.arange(WIN_W)[None, :]
    col_rel = col_idx - col[:, None] + (WIN_W - 1)
    rpb_c = rpb[:, :, col_rel]
    scale = HEAD_DIM ** -0.5

    def row_block(args):
        r, q_r = args
        start = jnp.clip(r - kh // 2, 0, rows - kh)
        key_rows = start + jnp.arange(kh)
        k_win = jnp.take(jnp.take(kg, key_rows, axis=2), col_idx, axis=3)
        v_win = jnp.take(jnp.take(vg, key_rows, axis=2), col_idx, axis=3)
        bias = jnp.take(rpb_c, key_rows - r + (WIN_H_MAX - 1), axis=1)
        bias = bias.transpose(0, 2, 1, 3).astype(jnp.float32)
        sc = jnp.einsum('bhqd,bhrqcd->bhqrc', q_r * scale, k_win).astype(jnp.float32) + bias[None]
        p = jax.nn.softmax(sc.reshape(b, h, GRID_W, kh * WIN_W), axis=-1)
        p = p.reshape(b, h, GRID_W, kh, WIN_W).astype(v_win.dtype)
        return jnp.einsum('bhqrc,bhrqcd->bhqd', p, v_win)

    out = lax.map(row_block, (jnp.arange(rows), qg))
    return out.transpose(1, 0, 3, 2, 4).reshape(b, s, h * dh)


def multiscale_pool(u, w_pool, pool_scale):
    b, s, _ = u.shape
    uf = u.reshape(b, s, N_POOL_GROUPS, POOL_GROUP).astype(jnp.float32)
    csum = jnp.concatenate([jnp.zeros((b, 1, N_POOL_GROUPS, POOL_GROUP), jnp.float32),
                            jnp.cumsum(uf, axis=1)], axis=1)
    t = np.arange(s)[:, None]
    w = np.array(POOL_WINDOWS)[None, :]
    lo = np.clip(t - w // 2, 0, s)
    hi = np.clip(t - w // 2 + w, 0, s)
    g = np.arange(N_POOL_GROUPS)[None, :]
    window_sum = csum[:, hi, g] - csum[:, lo, g]
    count = (hi - lo).astype(np.float32)[None, :, :, None]
    mixed = (window_sum / count - uf).astype(u.dtype)
    y = jnp.einsum('bsgc,gcd->bsgd', mixed, w_pool).reshape(b, s, POOL_WIDTH)
    return y * pool_scale


def setup_inputs(seed: int = 0) -> dict:
    key = jax.random.key(seed)
    ks = jax.random.split(key, 20)
    f32 = jnp.float32
    x = jax.random.normal(ks[0], (BATCH, SEQ, D_MODEL), f32)
    c = jax.random.normal(ks[1], (BATCH, D_MODEL), f32)
    ln_in_g = 1.0 + 0.02 * jax.random.normal(ks[2], (D_MODEL,), f32)
    ln_in_b = 0.02 * jax.random.normal(ks[3], (D_MODEL,), f32)
    w_ada = 0.1 * D_MODEL ** -0.5 * jax.random.normal(ks[4], (DEPTH, D_MODEL, N_MOD * D_MODEL), f32)
    b_ada = 0.01 * jax.random.normal(ks[5], (DEPTH, N_MOD * D_MODEL), f32)
    col_scale = np.concatenate([np.ones(2 * ATTN_WIDTH, np.float32),
                                np.full(ATTN_WIDTH, DN_BETA, np.float32),
                                np.ones(POOL_WIDTH, np.float32)])
    w_in = D_MODEL ** -0.5 * jax.random.normal(ks[6], (DEPTH, D_MODEL, PROJ_WIDTH), f32) * col_scale
    rpb = 0.1 * jax.random.normal(ks[7], (DEPTH, N_HEADS, 2 * WIN_H_MAX - 1, 2 * WIN_W - 1), f32)
    w_pool = POOL_GROUP ** -0.5 * jax.random.normal(ks[8], (DEPTH, N_POOL_GROUPS, POOL_GROUP, POOL_GROUP), f32)
    pool_scale = 1.0 + 0.1 * jax.random.normal(ks[9], (DEPTH, POOL_WIDTH), f32)
    w_out = DN_BETA * D_MODEL ** -0.5 * jax.random.normal(ks[10], (DEPTH, ATTN_WIDTH + POOL_WIDTH, D_MODEL), f32)
    ln1_g = 1.0 + 0.02 * jax.random.normal(ks[11], (DEPTH, D_MODEL), f32)
    ln1_b = 0.02 * jax.random.normal(ks[12], (DEPTH, D_MODEL), f32)
    w_mlp1 = D_MODEL ** -0.5 * jax.random.normal(ks[13], (DEPTH, D_MODEL, D_FF), f32)
    w_mlp2 = DN_BETA * D_FF ** -0.5 * jax.random.normal(ks[14], (DEPTH, D_FF, D_MODEL), f32)
    ln2_g = 1.0 + 0.02 * jax.random.normal(ks[15], (DEPTH, D_MODEL), f32)
    ln2_b = 0.02 * jax.random.normal(ks[16], (DEPTH, D_MODEL), f32)
    return {"x": x, "c": c, "ln_in_g": ln_in_g, "ln_in_b": ln_in_b, "w_ada": w_ada, "b_ada": b_ada,
            "w_in": w_in, "rpb": rpb, "w_pool": w_pool, "pool_scale": pool_scale, "w_out": w_out,
            "ln1_g": ln1_g, "ln1_b": ln1_b, "w_mlp1": w_mlp1, "w_mlp2": w_mlp2,
            "ln2_g": ln2_g, "ln2_b": ln2_b}


def reference(x, c, ln_in_g, ln_in_b, w_ada, b_ada, w_in, rpb, w_pool, pool_scale, w_out,
              ln1_g, ln1_b, w_mlp1, w_mlp2, ln2_g, ln2_b):
    b, s, _ = x.shape
    x = layer_norm(x, ln_in_g, ln_in_b)
    c_act = jax.nn.silu(c)
    for l in range(DEPTH):
        mod = (c_act @ w_ada[l] + b_ada[l])[:, None, :]
        sh_a, sc_a, g_a, sh_m, sc_m, g_m = jnp.split(mod, N_MOD, axis=-1)

        h = x * (1.0 + sc_a) + sh_a
        proj = h @ w_in[l]
        q, k, v, u = jnp.split(proj, [ATTN_WIDTH, 2 * ATTN_WIDTH, 3 * ATTN_WIDTH], axis=-1)
        q = q.reshape(b, s, N_HEADS, HEAD_DIM)
        k = k.reshape(b, s, N_HEADS, HEAD_DIM)
        v = v.reshape(b, s, N_HEADS, HEAD_DIM)
        y_attn = neighborhood_attention(q, k, v, rpb[l])
        y_pool = multiscale_pool(u, w_pool[l], pool_scale[l])
        y = jnp.concatenate([y_attn, y_pool], axis=-1) @ w_out[l]
        x = layer_norm(DN_ALPHA * x + (1.0 + g_a) * y, ln1_g[l], ln1_b[l])

        h = x * (1.0 + sc_m) + sh_m
        f = jnp.square(jax.nn.relu(h @ w_mlp1[l])) @ w_mlp2[l]
        x = layer_norm(DN_ALPHA * x + (1.0 + g_m) * f, ln2_g[l], ln2_b[l])
    return x
```

```python
import functools

import numpy as np
import jax
import jax.numpy as jnp
from jax import lax
from jax.experimental import pallas as pl
from jax.experimental.pallas import tpu as pltpu

GRID_W = 64
HEAD_DIM = 64
N_HEADS = 8
ATTN_WIDTH = N_HEADS * HEAD_DIM
WIN_H = 8
WIN_W = 16
POOL_WINDOWS = (2, 4, 8, 16)
POOL_GROUP = 128
POOL_WIDTH = POOL_GROUP * len(POOL_WINDOWS)
N_MOD = 6
LN_EPS = 1e-5
MASK_BIAS = -1e30

LANES = 128
HEADS_PER_GROUP = LANES // HEAD_DIM
N_HEAD_GROUPS = N_HEADS // HEADS_PER_GROUP
HALO = 16
VMEM_LIMIT = 56 * 1024 * 1024

F32 = jnp.float32
BF16 = jnp.bfloat16


def _layer_norm(x, g, b):
    mu = jnp.mean(x, axis=-1, keepdims=True)
    xc = x - mu
    var = jnp.mean(jnp.square(xc), axis=-1, keepdims=True)
    return xc * lax.rsqrt(var + LN_EPS) * g + b


def _mod_kernel(c_ref, w_ref, b_ref, o_ref):
    c = c_ref[...]
    c_act = (c * jax.nn.sigmoid(c)).astype(BF16)
    o_ref[0] = jnp.dot(c_act, w_ref[0].astype(BF16), preferred_element_type=F32) + b_ref[0]


def _modulation(c, w_ada, b_ada):
    depth, d, n = w_ada.shape
    b = c.shape[0]
    tn = 1024
    return pl.pallas_call(
        _mod_kernel,
        out_shape=jax.ShapeDtypeStruct((depth, b, n), F32),
        grid=(depth, n // tn),
        in_specs=[pl.BlockSpec((b, d), lambda l, j: (0, 0)),
                  pl.BlockSpec((1, d, tn), lambda l, j: (l, 0, j)),
                  pl.BlockSpec((1, 1, tn), lambda l, j: (l, 0, j))],
        out_specs=pl.BlockSpec((1, b, tn), lambda l, j: (l, 0, j)),
        compiler_params=pltpu.CompilerParams(dimension_semantics=("arbitrary", "arbitrary"),
                                             vmem_limit_bytes=VMEM_LIMIT),
        name="adaln_mod",
    )(c, w_ada, b_ada.reshape(depth, 1, n))


def _proj_kernel(*refs, first):
    if first:
        x_ref, mod_ref, g_ref, b_ref, w_ref, q_ref, k_ref, v_ref, u_ref, x0_ref = refs
    else:
        x_ref, mod_ref, w_ref, q_ref, k_ref, v_ref, u_ref = refs
    x = x_ref[0]
    if first:
        x = _layer_norm(x, g_ref[...], b_ref[...])
        x0_ref[0] = x
    m = mod_ref[0]
    h = x * (1.0 + m[1:2]) + m[0:1]
    p = jnp.dot(h.astype(BF16), w_ref[...], preferred_element_type=F32)
    a = ATTN_WIDTH
    q_ref[0] = (p[:, :a] * (HEAD_DIM ** -0.5)).astype(BF16)
    k_ref[0] = p[:, a:2 * a].astype(BF16)
    v_ref[0] = p[:, 2 * a:3 * a].astype(BF16)
    u_ref[0] = p[:, 3 * a:].astype(BF16)


def _project(x, mod, w_in, ln_g=None, ln_b=None, *, tm):
    b, s, d = x.shape
    first = ln_g is not None
    row = lambda bi, i: (bi, i, 0)
    const2 = lambda bi, i: (0, 0)
    in_specs = [pl.BlockSpec((1, tm, d), row),
                pl.BlockSpec((1, N_MOD, d), lambda bi, i: (bi, 0, 0))]
    args = [x, mod]
    if first:
        in_specs += [pl.BlockSpec((1, d), const2), pl.BlockSpec((1, d), const2)]
        args += [ln_g.reshape(1, d), ln_b.reshape(1, d)]
    in_specs.append(pl.BlockSpec(w_in.shape, const2))
    args.append(w_in)
    out_shape = [jax.ShapeDtypeStruct((b, s, ATTN_WIDTH), BF16)] * 3 + [jax.ShapeDtypeStruct((b, s, POOL_WIDTH), BF16)]
    out_specs = [pl.BlockSpec((1, tm, ATTN_WIDTH), row)] * 3 + [pl.BlockSpec((1, tm, POOL_WIDTH), row)]
    if first:
        out_shape.append(jax.ShapeDtypeStruct((b, s, d), F32))
        out_specs.append(pl.BlockSpec((1, tm, d), row))
    return pl.pallas_call(
        functools.partial(_proj_kernel, first=first),
        out_shape=out_shape,
        grid=(b, s // tm),
        in_specs=in_specs,
        out_specs=out_specs,
        compiler_params=pltpu.CompilerParams(dimension_semantics=("arbitrary", "arbitrary"),
                                             vmem_limit_bytes=VMEM_LIMIT),
        name="proj_first" if first else "proj",
    )(*args)


def _attn_bias_table(rpb):
    var = np.arange(WIN_H)[:, None]
    krow = np.arange(WIN_H)[None, :]
    dr = krow - var + (WIN_H - 1)
    qc = np.arange(GRID_W)[:, None]
    kc = np.arange(GRID_W)[None, :]
    cs = np.clip(qc - WIN_W // 2, 0, GRID_W - WIN_W)
    valid = (kc >= cs) & (kc < cs + WIN_W)
    cr = np.clip(kc - qc + (WIN_W - 1), 0, 2 * WIN_W - 2)
    t = rpb[:, dr[:, :, None, None], cr[None, None]]
    t = jnp.where(valid[None, None, None], t.astype(F32), MASK_BIAS)
    t = t.transpose(1, 0, 3, 2, 4)
    return t.reshape(WIN_H, N_HEAD_GROUPS, HEADS_PER_GROUP * GRID_W, WIN_H * GRID_W)


def _attn_kernel(q_ref, k_ref, v_ref, bias_ref, o_ref, *, rows_per_step, n_rows):
    j = pl.program_id(1)
    lane = lax.broadcasted_iota(jnp.int32, (GRID_W, LANES), 1)
    first_head = lane < HEAD_DIM
    zero = jnp.zeros((GRID_W, LANES), BF16)

    def row_body(rr, carry):
        r = j * rows_per_step + rr
        start = jnp.clip(r - WIN_H // 2, 0, n_rows - WIN_H)
        variant = r - start
        koff = pl.multiple_of(start * GRID_W, GRID_W)
        qoff = pl.multiple_of(rr * GRID_W, GRID_W)
        for g in range(N_HEAD_GROUPS):
            cols = slice(g * LANES, (g + 1) * LANES)
            q = q_ref[0, pl.ds(qoff, GRID_W), cols]
            k = k_ref[0, pl.ds(koff, WIN_H * GRID_W), cols]
            v = v_ref[0, pl.ds(koff, WIN_H * GRID_W), cols]
            q2 = jnp.concatenate([jnp.where(first_head, q, zero), jnp.where(first_head, zero, q)], axis=0)
            s = lax.dot_general(q2, k, (((1,), (1,)), ((), ())), preferred_element_type=F32)
            s = s + bias_ref[variant, g]
            e = jnp.exp(s - jnp.max(s, axis=-1, keepdims=True))
            denom = jnp.sum(e, axis=-1, keepdims=True)
            o2 = jnp.dot(e.astype(BF16), v, preferred_element_type=F32) / denom
            o = jnp.where(first_head, o2[:GRID_W], o2[GRID_W:])
            o_ref[0, pl.ds(qoff, GRID_W), cols] = o.astype(BF16)
        return carry

    lax.fori_loop(0, rows_per_step, row_body, 0)


def _attention(q, k, v, bias, *, rows_per_step):
    b, s, a = q.shape
    n_rows = s // GRID_W
    assert n_rows >= WIN_H and n_rows % rows_per_step == 0
    tq = rows_per_step * GRID_W
    return pl.pallas_call(
        functools.partial(_attn_kernel, rows_per_step=rows_per_step, n_rows=n_rows),
        out_shape=jax.ShapeDtypeStruct((b, s, a), BF16),
        grid=(b, n_rows // rows_per_step),
        in_specs=[pl.BlockSpec((1, tq, a), lambda bi, j: (bi, j, 0)),
                  pl.BlockSpec((1, s, a), lambda bi, j: (bi, 0, 0)),
                  pl.BlockSpec((1, s, a), lambda bi, j: (bi, 0, 0)),
                  pl.BlockSpec(bias.shape, lambda bi, j: (0, 0, 0, 0))],
        out_specs=pl.BlockSpec((1, tq, a), lambda bi, j: (bi, j, 0)),
        compiler_params=pltpu.CompilerParams(dimension_semantics=("arbitrary", "arbitrary"),
                                             vmem_limit_bytes=VMEM_LIMIT),
        name="nbr_attn",
    )(q, k, v, bias)


def _mix_kernel(x_ref, ya_ref, u_ref, up_ref, un_ref, mod_ref, wp_ref, ps_ref, wo_ref,
                g1_ref, b1_ref, w1_ref, w2_ref, g2_ref, b2_ref, o_ref, uext_ref,
                *, tm, seq, alpha, ff_chunk):
    i = pl.program_id(1)
    nt = pl.num_programs(1)
    m = mod_ref[0]

    uext_ref[0:HALO] = jnp.where(i > 0, up_ref[0].astype(F32), 0.0)
    uext_ref[HALO:HALO + tm] = u_ref[0].astype(F32)
    uext_ref[HALO + tm:] = jnp.where(i < nt - 1, un_ref[0].astype(F32), 0.0)
    t = i * tm + lax.broadcasted_iota(jnp.int32, (tm, 1), 0)
    pooled = []
    for g, w in enumerate(POOL_WINDOWS):
        cols = slice(g * POOL_GROUP, (g + 1) * POOL_GROUP)
        acc = uext_ref[HALO - w // 2:HALO - w // 2 + tm, cols]
        for jj in range(1, w):
            off = HALO - w // 2 + jj
            acc = acc + uext_ref[off:off + tm, cols]
        lo = jnp.maximum(t - w // 2, 0)
        hi = jnp.minimum(t - w // 2 + w, seq)
        cnt = (hi - lo).astype(F32)
        mixed = (acc / cnt - uext_ref[HALO:HALO + tm, cols]).astype(BF16)
        yg = jnp.dot(mixed, wp_ref[g], preferred_element_type=F32) * ps_ref[:, cols]
        pooled.append(yg.astype(BF16))
    y_pool = jnp.concatenate(pooled, axis=-1)

    y = jnp.dot(ya_ref[0], wo_ref[:ATTN_WIDTH], preferred_element_type=F32)
    y = y + jnp.dot(y_pool, wo_ref[ATTN_WIDTH:], preferred_element_type=F32)
    x1 = _layer_norm(alpha * x_ref[0] + (1.0 + m[2:3]) * y, g1_ref[...], b1_ref[...])

    h = (x1 * (1.0 + m[4:5]) + m[3:4]).astype(BF16)
    d_ff = w1_ref.shape[1]
    f = None
    for c0 in range(0, d_ff, ff_chunk):
        a = jnp.dot(h, w1_ref[:, c0:c0 + ff_chunk], preferred_element_type=F32)
        a = jnp.square(jnp.maximum(a, 0.0)).astype(BF16)
        part = jnp.dot(a, w2_ref[c0:c0 + ff_chunk], preferred_element_type=F32)
        f = part if f is None else f + part
    o_ref[0] = _layer_norm(alpha * x1 + (1.0 + m[5:6]) * f, g2_ref[...], b2_ref[...])


def _mix(x, y_attn, u, mod, w_pool, pool_scale, w_out, g1, b1, w1, w2, g2, b2, *, tm, alpha):
    b, s, d = x.shape
    assert s % tm == 0 and tm % HALO == 0
    hb = tm // HALO
    n_halo = s // HALO
    row = lambda bi, i: (bi, i, 0)
    c2 = lambda bi, i: (0, 0)
    c3 = lambda bi, i: (0, 0, 0)
    vec = lambda a: a.reshape(1, -1)
    once = pl.Buffered(1)
    in_specs = [
        pl.BlockSpec((1, tm, d), row),
        pl.BlockSpec((1, tm, ATTN_WIDTH), row),
        pl.BlockSpec((1, tm, POOL_WIDTH), row),
        pl.BlockSpec((1, HALO, POOL_WIDTH), lambda bi, i: (bi, jnp.maximum(i * hb - 1, 0), 0)),
        pl.BlockSpec((1, HALO, POOL_WIDTH), lambda bi, i: (bi, jnp.minimum((i + 1) * hb, n_halo - 1), 0)),
        pl.BlockSpec((1, N_MOD, d), lambda bi, i: (bi, 0, 0)),
        pl.BlockSpec(w_pool.shape, c3, pipeline_mode=once),
        pl.BlockSpec((1, POOL_WIDTH), c2, pipeline_mode=once),
        pl.BlockSpec(w_out.shape, c2, pipeline_mode=once),
        pl.BlockSpec((1, d), c2, pipeline_mode=once),
        pl.BlockSpec((1, d), c2, pipeline_mode=once),
        pl.BlockSpec(w1.shape, c2, pipeline_mode=once),
        pl.BlockSpec(w2.shape, c2, pipeline_mode=once),
        pl.BlockSpec((1, d), c2, pipeline_mode=once),
        pl.BlockSpec((1, d), c2, pipeline_mode=once),
    ]
    return pl.pallas_call(
        functools.partial(_mix_kernel, tm=tm, seq=s, alpha=alpha, ff_chunk=1024),
        out_shape=jax.ShapeDtypeStruct((b, s, d), F32),
        grid=(b, s // tm),
        in_specs=in_specs,
        out_specs=pl.BlockSpec((1, tm, d), row),
        scratch_shapes=[pltpu.VMEM((tm + 2 * HALO, POOL_WIDTH), F32)],
        compiler_params=pltpu.CompilerParams(dimension_semantics=("arbitrary", "arbitrary"),
                                             vmem_limit_bytes=VMEM_LIMIT),
        name="mix_mlp",
    )(x, y_attn, u, u, u, mod, w_pool, vec(pool_scale), w_out, vec(g1), vec(b1), w1, w2, vec(g2), vec(b2))


def kernel(x, c, ln_in_g, ln_in_b, w_ada, b_ada, w_in, rpb, w_pool, pool_scale, w_out,
           ln1_g, ln1_b, w_mlp1, w_mlp2, ln2_g, ln2_b):
    b, s, d = x.shape
    depth = w_in.shape[0]
    alpha = (2.0 * depth) ** 0.25
    tm = min(512, s)
    mod = _modulation(c, w_ada, b_ada).reshape(depth, b, N_MOD, d)
    for l in range(depth):
        w_in_l = w_in[l].astype(BF16)
        if l == 0:
            q, k, v, u, x = _project(x, mod[l], w_in_l, ln_in_g, ln_in_b, tm=tm)
        else:
            q, k, v, u = _project(x, mod[l], w_in_l, tm=tm)
        y_attn = _attention(q, k, v, _attn_bias_table(rpb[l]), rows_per_step=8)
        x = _mix(x, y_attn, u, mod[l], w_pool[l].astype(BF16), pool_scale[l], w_out[l].astype(BF16),
                 ln1_g[l], ln1_b[l], w_mlp1[l].astype(BF16), w_mlp2[l].astype(BF16), ln2_g[l], ln2_b[l],
                 tm=tm, alpha=alpha)
    return x
```

```python
import functools

import numpy as np
import jax
import jax.numpy as jnp
from jax import lax
from jax.experimental import pallas as pl
from jax.experimental.pallas import tpu as pltpu

GRID_W = 64
HEAD_DIM = 64
N_HEADS = 8
ATTN_WIDTH = N_HEADS * HEAD_DIM
WIN_H = 8
WIN_W = 16
POOL_WINDOWS = (2, 4, 8, 16)
POOL_GROUP = 128
POOL_WIDTH = POOL_GROUP * len(POOL_WINDOWS)
N_MOD = 6
LN_EPS = 1e-5
MASK_BIAS = -1e30

LANES = 128
HEADS_PER_GROUP = LANES // HEAD_DIM
N_HEAD_GROUPS = N_HEADS // HEADS_PER_GROUP
HALO = 16
VMEM_LIMIT = 56 * 1024 * 1024

F32 = jnp.float32
BF16 = jnp.bfloat16


def _layer_norm(x, g, b):
    mu = jnp.mean(x, axis=-1, keepdims=True)
    xc = x - mu
    var = jnp.mean(jnp.square(xc), axis=-1, keepdims=True)
    return xc * lax.rsqrt(var + LN_EPS) * g + b


def _mod_kernel(c_ref, w_ref, b_ref, o_ref):
    c = c_ref[...]
    c_act = (c * jax.nn.sigmoid(c)).astype(BF16)
    o_ref[0] = jnp.dot(c_act, w_ref[0].astype(BF16), preferred_element_type=F32) + b_ref[0]


def _modulation(c, w_ada, b_ada):
    depth, d, n = w_ada.shape
    b = c.shape[0]
    tn = 1024
    return pl.pallas_call(
        _mod_kernel,
        out_shape=jax.ShapeDtypeStruct((depth, b, n), F32),
        grid=(depth, n // tn),
        in_specs=[pl.BlockSpec((b, d), lambda l, j: (0, 0)),
                  pl.BlockSpec((1, d, tn), lambda l, j: (l, 0, j)),
                  pl.BlockSpec((1, 1, tn), lambda l, j: (l, 0, j))],
        out_specs=pl.BlockSpec((1, b, tn), lambda l, j: (l, 0, j)),
        compiler_params=pltpu.CompilerParams(dimension_semantics=("arbitrary", "arbitrary"),
                                             vmem_limit_bytes=VMEM_LIMIT),
        name="adaln_mod",
    )(c, w_ada, b_ada.reshape(depth, 1, n))


def _proj_kernel(*refs, first):
    if first:
        x_ref, mod_ref, g_ref, b_ref, w_ref, q_ref, k_ref, v_ref, u_ref, x0_ref = refs
    else:
        x_ref, mod_ref, w_ref, q_ref, k_ref, v_ref, u_ref = refs
    x = x_ref[0]
    if first:
        x = _layer_norm(x, g_ref[...], b_ref[...])
        x0_ref[0] = x
    m = mod_ref[0]
    h = x * (1.0 + m[1:2]) + m[0:1]
    p = jnp.dot(h.astype(BF16), w_ref[...], preferred_element_type=F32)
    a = ATTN_WIDTH
    q_ref[0] = (p[:, :a] * (HEAD_DIM ** -0.5)).astype(BF16)
    k_ref[0] = p[:, a:2 * a].astype(BF16)
    v_ref[0] = p[:, 2 * a:3 * a].astype(BF16)
    u_ref[0] = p[:, 3 * a:].astype(BF16)


def _project(x, mod, w_in, ln_g=None, ln_b=None, *, tm):
    b, s, d = x.shape
    first = ln_g is not None
    row = lambda bi, i: (bi, i, 0)
    const2 = lambda bi, i: (0, 0)
    in_specs = [pl.BlockSpec((1, tm, d), row),
                pl.BlockSpec((1, N_MOD, d), lambda bi, i: (bi, 0, 0))]
    args = [x, mod]
    if first:
        in_specs += [pl.BlockSpec((1, d), const2), pl.BlockSpec((1, d), const2)]
        args += [ln_g.reshape(1, d), ln_b.reshape(1, d)]
    in_specs.append(pl.BlockSpec(w_in.shape, const2))
    args.append(w_in)
    out_shape = [jax.ShapeDtypeStruct((b, s, ATTN_WIDTH), BF16)] * 3 + [jax.ShapeDtypeStruct((b, s, POOL_WIDTH), BF16)]
    out_specs = [pl.BlockSpec((1, tm, ATTN_WIDTH), row)] * 3 + [pl.BlockSpec((1, tm, POOL_WIDTH), row)]
    if first:
        out_shape.append(jax.ShapeDtypeStruct((b, s, d), F32))
        out_specs.append(pl.BlockSpec((1, tm, d), row))
    return pl.pallas_call(
        functools.partial(_proj_kernel, first=first),
        out_shape=out_shape,
        grid=(b, s // tm),
        in_specs=in_specs,
        out_specs=out_specs,
        compiler_params=pltpu.CompilerParams(dimension_semantics=("arbitrary", "arbitrary"),
                                             vmem_limit_bytes=VMEM_LIMIT),
        name="proj_first" if first else "proj",
    )(*args)


def _attn_bias_table(rpb):
    qc = np.arange(GRID_W)[:, None]
    kc = np.arange(GRID_W)[None, :]
    cs = np.clip(qc - WIN_W // 2, 0, GRID_W - WIN_W)
    valid = (kc >= cs) & (kc < cs + WIN_W)
    n_rel = 2 * WIN_W - 1
    onehot = ((kc - qc + (WIN_W - 1))[None] == np.arange(n_rel)[:, None, None]) & valid[None]
    base = jnp.einsum("hdc,cqk->hqdk", rpb.astype(F32), jnp.asarray(onehot, F32),
                      precision=lax.Precision.HIGHEST)
    base = jnp.where(valid[None, :, None, :], base, MASK_BIAS)
    base = base.reshape(N_HEADS, GRID_W, (2 * WIN_H - 1) * GRID_W)
    t = jnp.stack([base[:, :, (WIN_H - 1 - var) * GRID_W:(2 * WIN_H - 1 - var) * GRID_W]
                   for var in range(WIN_H)])
    return t.reshape(WIN_H, N_HEAD_GROUPS, HEADS_PER_GROUP * GRID_W, WIN_H * GRID_W)


def _attn_kernel(q_ref, k_ref, v_ref, bias_ref, o_ref, s_scr, p_scr, r_scr, *, n_rows):
    lane = lax.broadcasted_iota(jnp.int32, (GRID_W, LANES), 1)
    first_head = lane < HEAD_DIM
    zero = jnp.zeros((GRID_W, LANES), BF16)
    win = WIN_H * GRID_W

    def key_offset(r):
        start = jnp.clip(r - WIN_H // 2, 0, n_rows - WIN_H)
        return start, pl.multiple_of(start * GRID_W, GRID_W)

    def scores(r, slot):
        start, koff = key_offset(r)
        qoff = pl.multiple_of(r * GRID_W, GRID_W)
        for g in range(N_HEAD_GROUPS):
            cols = slice(g * LANES, (g + 1) * LANES)
            q = q_ref[0, pl.ds(qoff, GRID_W), cols]
            k = k_ref[0, pl.ds(koff, win), cols]
            q2 = jnp.concatenate([jnp.where(first_head, q, zero), jnp.where(first_head, zero, q)], axis=0)
            s = lax.dot_general(q2, k, (((1,), (1,)), ((), ())), preferred_element_type=F32)
            s_scr[slot, g] = s + bias_ref[r - start, g]

    def softmax(slot):
        for g in range(N_HEAD_GROUPS):
            s = s_scr[slot, g]
            e = jnp.exp(s - jnp.max(s, axis=-1, keepdims=True))
            p_scr[slot, g] = e.astype(BF16)
            r_scr[slot, g] = jnp.broadcast_to(1.0 / jnp.sum(e, axis=-1, keepdims=True), (2 * GRID_W, LANES))

    def output(r, slot):
        _, koff = key_offset(r)
        qoff = pl.multiple_of(r * GRID_W, GRID_W)
        for g in range(N_HEAD_GROUPS):
            cols = slice(g * LANES, (g + 1) * LANES)
            v = v_ref[0, pl.ds(koff, win), cols]
            o2 = jnp.dot(p_scr[slot, g], v, preferred_element_type=F32) * r_scr[slot, g]
            o = jnp.where(first_head, o2[:GRID_W], o2[GRID_W:])
            o_ref[0, pl.ds(qoff, GRID_W), cols] = o.astype(BF16)

    scores(0, 0)
    softmax(0)
    scores(1, 1)

    def two_rows(t, carry):
        r = 2 * t
        output(r - 2, 0)
        softmax(1)
        scores(r, 0)
        output(r - 1, 1)
        softmax(0)
        scores(r + 1, 1)
        return carry

    lax.fori_loop(1, n_rows // 2, two_rows, 0)
    output(n_rows - 2, 0)
    softmax(1)
    output(n_rows - 1, 1)


def _attention(q, k, v, bias):
    b, s, a = q.shape
    n_rows = s // GRID_W
    assert n_rows >= WIN_H and n_rows % 2 == 0
    whole = lambda bi: (bi, 0, 0)
    tile = (N_HEAD_GROUPS, HEADS_PER_GROUP * GRID_W)
    return pl.pallas_call(
        functools.partial(_attn_kernel, n_rows=n_rows),
        out_shape=jax.ShapeDtypeStruct((b, s, a), BF16),
        grid=(b,),
        in_specs=[pl.BlockSpec((1, s, a), whole),
                  pl.BlockSpec((1, s, a), whole),
                  pl.BlockSpec((1, s, a), whole),
                  pl.BlockSpec(bias.shape, lambda bi: (0, 0, 0, 0), pipeline_mode=pl.Buffered(1))],
        out_specs=pl.BlockSpec((1, s, a), whole),
        scratch_shapes=[pltpu.VMEM((2,) + tile + (WIN_H * GRID_W,), F32),
                        pltpu.VMEM((2,) + tile + (WIN_H * GRID_W,), BF16),
                        pltpu.VMEM((2,) + tile + (LANES,), F32)],
        compiler_params=pltpu.CompilerParams(dimension_semantics=("arbitrary",),
                                             vmem_limit_bytes=VMEM_LIMIT),
        name="nbr_attn",
    )(q, k, v, bias)


def _mix_kernel(x_ref, ya_ref, u_ref, up_ref, un_ref, mod_ref, wp_ref, ps_ref, wo_ref,
                g1_ref, b1_ref, w1_ref, w2_ref, g2_ref, b2_ref, o_ref, uext_ref,
                *, tm, seq, alpha, ff_chunk):
    i = pl.program_id(1)
    nt = pl.num_programs(1)
    m = mod_ref[0]

    uext_ref[0:HALO] = jnp.where(i > 0, up_ref[0].astype(F32), 0.0)
    uext_ref[HALO:HALO + tm] = u_ref[0].astype(F32)
    uext_ref[HALO + tm:] = jnp.where(i < nt - 1, un_ref[0].astype(F32), 0.0)
    t = i * tm + lax.broadcasted_iota(jnp.int32, (tm, 1), 0)
    pooled = []
    for g, w in enumerate(POOL_WINDOWS):
        cols = slice(g * POOL_GROUP, (g + 1) * POOL_GROUP)
        acc = uext_ref[HALO - w // 2:HALO - w // 2 + tm, cols]
        for jj in range(1, w):
            off = HALO - w // 2 + jj
            acc = acc + uext_ref[off:off + tm, cols]
        lo = jnp.maximum(t - w // 2, 0)
        hi = jnp.minimum(t - w // 2 + w, seq)
        cnt = (hi - lo).astype(F32)
        mixed = (acc / cnt - uext_ref[HALO:HALO + tm, cols]).astype(BF16)
        yg = jnp.dot(mixed, wp_ref[g], preferred_element_type=F32) * ps_ref[:, cols]
        pooled.append(yg.astype(BF16))
    y_pool = jnp.concatenate(pooled, axis=-1)

    y = jnp.dot(ya_ref[0], wo_ref[:ATTN_WIDTH], preferred_element_type=F32)
    y = y + jnp.dot(y_pool, wo_ref[ATTN_WIDTH:], preferred_element_type=F32)
    x1 = _layer_norm(alpha * x_ref[0] + (1.0 + m[2:3]) * y, g1_ref[...], b1_ref[...])

    h = (x1 * (1.0 + m[4:5]) + m[3:4]).astype(BF16)
    d_ff = w1_ref.shape[1]
    f = None
    for c0 in range(0, d_ff, ff_chunk):
        a = jnp.dot(h, w1_ref[:, c0:c0 + ff_chunk], preferred_element_type=F32)
        a = jnp.square(jnp.maximum(a, 0.0)).astype(BF16)
        part = jnp.dot(a, w2_ref[c0:c0 + ff_chunk], preferred_element_type=F32)
        f = part if f is None else f + part
    o_ref[0] = _layer_norm(alpha * x1 + (1.0 + m[5:6]) * f, g2_ref[...], b2_ref[...])


def _mix(x, y_attn, u, mod, w_pool, pool_scale, w_out, g1, b1, w1, w2, g2, b2, *, tm, alpha):
    b, s, d = x.shape
    assert s % tm == 0 and tm % HALO == 0
    hb = tm // HALO
    n_halo = s // HALO
    row = lambda bi, i: (bi, i, 0)
    c2 = lambda bi, i: (0, 0)
    c3 = lambda bi, i: (0, 0, 0)
    vec = lambda a: a.reshape(1, -1)
    once = pl.Buffered(1)
    in_specs = [
        pl.BlockSpec((1, tm, d), row),
        pl.BlockSpec((1, tm, ATTN_WIDTH), row),
        pl.BlockSpec((1, tm, POOL_WIDTH), row),
        pl.BlockSpec((1, HALO, POOL_WIDTH), lambda bi, i: (bi, jnp.maximum(i * hb - 1, 0), 0)),
        pl.BlockSpec((1, HALO, POOL_WIDTH), lambda bi, i: (bi, jnp.minimum((i + 1) * hb, n_halo - 1), 0)),
        pl.BlockSpec((1, N_MOD, d), lambda bi, i: (bi, 0, 0)),
        pl.BlockSpec(w_pool.shape, c3, pipeline_mode=once),
        pl.BlockSpec((1, POOL_WIDTH), c2, pipeline_mode=once),
        pl.BlockSpec(w_out.shape, c2, pipeline_mode=once),
        pl.BlockSpec((1, d), c2, pipeline_mode=once),
        pl.BlockSpec((1, d), c2, pipeline_mode=once),
        pl.BlockSpec(w1.shape, c2, pipeline_mode=once),
        pl.BlockSpec(w2.shape, c2, pipeline_mode=once),
        pl.BlockSpec((1, d), c2, pipeline_mode=once),
        pl.BlockSpec((1, d), c2, pipeline_mode=once),
    ]
    return pl.pallas_call(
        functools.partial(_mix_kernel, tm=tm, seq=s, alpha=alpha, ff_chunk=1024),
        out_shape=jax.ShapeDtypeStruct((b, s, d), F32),
        grid=(b, s // tm),
        in_specs=in_specs,
        out_specs=pl.BlockSpec((1, tm, d), row),
        scratch_shapes=[pltpu.VMEM((tm + 2 * HALO, POOL_WIDTH), F32)],
        compiler_params=pltpu.CompilerParams(dimension_semantics=("arbitrary", "arbitrary"),
                                             vmem_limit_bytes=VMEM_LIMIT),
        name="mix_mlp",
    )(x, y_attn, u, u, u, mod, w_pool, vec(pool_scale), w_out, vec(g1), vec(b1), w1, w2, vec(g2), vec(b2))


def kernel(x, c, ln_in_g, ln_in_b, w_ada, b_ada, w_in, rpb, w_pool, pool_scale, w_out,
           ln1_g, ln1_b, w_mlp1, w_mlp2, ln2_g, ln2_b):
    b, s, d = x.shape
    depth = w_in.shape[0]
    alpha = (2.0 * depth) ** 0.25
    tm = min(512, s)
    mod = _modulation(c, w_ada, b_ada).reshape(depth, b, N_MOD, d)
    for l in range(depth):
        w_in_l = w_in[l].astype(BF16)
        if l == 0:
            q, k, v, u, x = _project(x, mod[l], w_in_l, ln_in_g, ln_in_b, tm=tm)
        else:
            q, k, v, u = _project(x, mod[l], w_in_l, tm=tm)
        y_attn = _attention(q, k, v, _attn_bias_table(rpb[l]))
        x = _mix(x, y_attn, u, mod[l], w_pool[l].astype(BF16), pool_scale[l], w_out[l].astype(BF16),
                 ln1_g[l], ln1_b[l], w_mlp1[l].astype(BF16), w_mlp2[l].astype(BF16), ln2_g[l], ln2_b[l],
                 tm=tm, alpha=alpha)
    return x
```
